```python
import jax, jax.numpy as jnp
from jax import lax
import numpy as np

D_MODEL = 2048
BATCH = 4
SEQ = 2048
DEPTH = 2

BRANCH_W = D_MODEL // 2
N_BRANCH = 3
Q_BLOCK = 128
ROPE_THETA = 10000.0
HEAD_DIM = 128
MOBA_HEADS = BRANCH_W // HEAD_DIM
MOBA_BLOCK = 256
MOBA_TOPK = 3
RWKV_HEAD = 64
RWKV_HEADS = BRANCH_W // RWKV_HEAD
RWKV_W = RWKV_HEADS * RWKV_HEAD
DECAY_LORA = 64
ICLR_LORA = 64
VRES_LORA = 32
GATE_LORA = 160
RWKV_GN_EPS = 64e-5
QK_NOPE = 128
QK_ROPE = 64
V_HEAD = 128
MLA_HEADS = BRANCH_W // V_HEAD
Q_LORA = 512
KV_LORA = 512
D_FF = 5632
N_EXPERTS = 8
TOP_K = 2
LN_EPS = 1e-5
RMS_EPS = 1e-6
DEEPNORM_ALPHA = (2 * DEPTH) ** 0.25
DEEPNORM_BETA = (8 * DEPTH) ** -0.25
N_DENSE = (DEPTH + 1) // 2
N_MOE = DEPTH // 2
RWKV_SPLIT = (RWKV_W, RWKV_W, RWKV_W, DECAY_LORA, ICLR_LORA, GATE_LORA)
RWKV_COLS = 3 * RWKV_W + DECAY_LORA + ICLR_LORA + GATE_LORA
MLA_COLS = Q_LORA + KV_LORA + QK_ROPE
IN_COLS = 3 * BRANCH_W + RWKV_COLS + MLA_COLS + N_BRANCH * D_MODEL

kernel_name = 'moba_rwkv7_mla_gated_hybrid_deepnorm_moe'


def split_cols(h, sizes):
    return jnp.split(h, np.cumsum(sizes)[:-1].tolist(), axis=-1)


def layer_norm(x, g, b):
    xf = x.astype(jnp.float32)
    mu = xf.mean(-1, keepdims=True)
    var = jnp.square(xf - mu).mean(-1, keepdims=True)
    return ((xf - mu) * lax.rsqrt(var + LN_EPS) * g + b).astype(x.dtype)


def rms_norm(x, g):
    xf = x.astype(jnp.float32)
    return (xf * lax.rsqrt(jnp.square(xf).mean(-1, keepdims=True) + RMS_EPS) * g).astype(x.dtype)


def rope_tables(seq, dim):
    inv = 1.0 / (ROPE_THETA ** (jnp.arange(0, dim, 2, dtype=jnp.float32) / dim))
    ang = jnp.arange(seq, dtype=jnp.float32)[:, None] * inv[None, :]
    return jnp.cos(ang), jnp.sin(ang)


def apply_rope(x, cos, sin):
    x1, x2 = jnp.split(x.astype(jnp.float32), 2, axis=-1)
    c = cos[None, :, None, :]
    s = sin[None, :, None, :]
    return jnp.concatenate([x1 * c - x2 * s, x2 * c + x1 * s], axis=-1).astype(x.dtype)


def token_shift_mix(h, mu):
    prev = jnp.pad(h, ((0, 0), (1, 0), (0, 0)))[:, :-1]
    return h + (prev - h) * mu


def moba_attention(q, k, v):
    B, S, H, Dh = q.shape
    nblk = -(-S // MOBA_BLOCK)
    sp = nblk * MOBA_BLOCK
    topk = min(MOBA_TOPK, nblk)
    scale = Dh ** -0.5
    pad = ((0, 0), (0, sp - S), (0, 0), (0, 0))
    q, k, v = (jnp.pad(t, pad).transpose(0, 2, 1, 3) for t in (q, k, v))
    kb = k.reshape(B, H, nblk, MOBA_BLOCK, Dh)
    vb = v.reshape(B, H, nblk, MOBA_BLOCK, Dh)
    k_mean = kb.astype(jnp.float32).mean(axis=3)
    gather = jax.vmap(jax.vmap(lambda blocks, idx: blocks[idx]))
    blk_ids = jnp.arange(nblk)
    n_sel = topk * MOBA_BLOCK

    def one_block(i):
        start = i * Q_BLOCK
        qc = lax.dynamic_slice_in_dim(q, start, Q_BLOCK, axis=2)
        qpos = start + jnp.arange(Q_BLOCK)
        own = start // MOBA_BLOCK
        gate = jnp.einsum('bhqd,bhnd->bhqn', qc.astype(jnp.float32), k_mean)
        fully_past = blk_ids[None, :] < (qpos // MOBA_BLOCK)[:, None]
        gate = jnp.where(fully_past, gate, -jnp.inf)
        g_val, g_idx = lax.top_k(gate, topk)
        k_sel = gather(kb, g_idx)
        v_sel = gather(vb, g_idx)
        s_sel = jnp.einsum('bhqd,bhqtkd->bhqtk', qc, k_sel).astype(jnp.float32) * scale
        s_sel = jnp.where(jnp.isfinite(g_val)[..., None], s_sel, -jnp.inf)
        k_own = lax.dynamic_slice_in_dim(k, own * MOBA_BLOCK, MOBA_BLOCK, axis=2)
        v_own = lax.dynamic_slice_in_dim(v, own * MOBA_BLOCK, MOBA_BLOCK, axis=2)
        s_own = jnp.einsum('bhqd,bhkd->bhqk', qc, k_own).astype(jnp.float32) * scale
        kpos = own * MOBA_BLOCK + jnp.arange(MOBA_BLOCK)
        s_own = jnp.where(kpos[None, :] <= qpos[:, None], s_own, -jnp.inf)
        s = jnp.concatenate([s_sel.reshape(B, H, Q_BLOCK, n_sel), s_own], axis=-1)
        p = jax.nn.softmax(s, axis=-1).astype(v.dtype)
        p_sel = p[..., :n_sel].reshape(B, H, Q_BLOCK, topk, MOBA_BLOCK)
        return (jnp.einsum('bhqtk,bhqtkd->bhqd', p_sel, v_sel)
                + jnp.einsum('bhqk,bhkd->bhqd', p[..., n_sel:], v_own))

    out = lax.map(one_block, jnp.arange(sp // Q_BLOCK))
    return out.transpose(1, 0, 3, 2, 4).reshape(B, sp, H, Dh)[:, :S]


def causal_attention(q, k, v):
    B, S, H, Dq = q.shape
    Dv = v.shape[-1]
    scale = Dq ** -0.5
    nq = S // Q_BLOCK
    qb = q.reshape(B, nq, Q_BLOCK, H, Dq).transpose(1, 0, 3, 2, 4)
    kpos = jnp.arange(S)

    def one_block(args):
        i, qc = args
        s = jnp.einsum('bhqd,bkhd->bhqk', qc, k).astype(jnp.float32) * scale
        qpos = i * Q_BLOCK + jnp.arange(Q_BLOCK)
        s = jnp.where(kpos[None, :] <= qpos[:, None], s, -jnp.inf)
        p = jax.nn.softmax(s, axis=-1).astype(v.dtype)
        return jnp.einsum('bhqk,bkhd->bqhd', p, v)

    out = lax.map(one_block, (jnp.arange(nq), qb))
    return out.transpose(1, 0, 2, 3, 4).reshape(B, S, H, Dv)


def wkv7_scan(r, decay, k, v, kk, a):
    B, S, H, N = r.shape

    def step(state, inp):
        r_t, w_t, k_t, v_t, kk_t, a_t = inp
        sa = jnp.einsum('bhij,bhj->bhi', state, -kk_t)
        state = (state * w_t[:, :, None, :] + sa[..., None] * (kk_t * a_t)[:, :, None, :]
                 + v_t[..., None] * k_t[:, :, None, :])
        return state, jnp.einsum('bhij,bhj->bhi', state, r_t)

    xs = tuple(t.transpose(1, 0, 2, 3) for t in (r, decay, k, v, kk, a))
    _, ys = lax.scan(step, jnp.zeros((B, H, N, N), jnp.float32), xs)
    return ys.transpose(1, 0, 2, 3)


def rwkv7_time_mix(h, w0, w_lora, a0, a_lora, g_lora, k_k, k_a, r_k, gn_g, gn_b,
                   v_first, xv_lo, v0, v_lora):
    B, S, _ = h.shape
    f32 = jnp.float32
    r, k, v, xw, xa, xg = split_cols(h, RWKV_SPLIT)
    w_log = -jax.nn.softplus(-(w0 + jnp.tanh(xw) @ w_lora)) - 0.5
    decay = jnp.exp(-jnp.exp(w_log.astype(f32)))
    a = jax.nn.sigmoid(a0 + xa @ a_lora)
    g = jax.nn.sigmoid(xg) @ g_lora
    v_raw = v
    if v_first is not None:
        v = v + (v_first - v) * jax.nn.sigmoid(v0 + xv_lo @ v_lora)
    heads = lambda t: t.reshape(B, S, RWKV_HEADS, RWKV_HEAD).astype(f32)
    r, k, v, a, decay = heads(r), heads(k), heads(v), heads(a), heads(decay)
    kk = k * k_k.reshape(RWKV_HEADS, RWKV_HEAD).astype(f32)
    kk = kk / jnp.maximum(jnp.sqrt(jnp.sum(kk * kk, axis=-1, keepdims=True)), 1e-12)
    k = k * (1.0 + (a - 1.0) * k_a.reshape(RWKV_HEADS, RWKV_HEAD).astype(f32))
    y = wkv7_scan(r, decay, k, v, kk, a)
    mu = y.mean(-1, keepdims=True)
    var = jnp.square(y - mu).mean(-1, keepdims=True)
    y = ((y - mu) * lax.rsqrt(var + RWKV_GN_EPS)).reshape(B, S, RWKV_W) * gn_g + gn_b
    bonus = (jnp.sum(r * k * r_k.astype(f32), axis=-1, keepdims=True) * v).reshape(B, S, RWKV_W)
    return ((y + bonus) * g).astype(h.dtype), v_raw


def mla_attention(q_dn, kv_dn, k_pe, q_norm, w_uq, kv_norm, w_ukv, cos, sin):
    B, S, _ = q_dn.shape
    q = (rms_norm(q_dn, q_norm) @ w_uq).reshape(B, S, MLA_HEADS, QK_NOPE + QK_ROPE)
    q_nope, q_pe = jnp.split(q, [QK_NOPE], axis=-1)
    kv = (rms_norm(kv_dn, kv_norm) @ w_ukv).reshape(B, S, MLA_HEADS, QK_NOPE + V_HEAD)
    k_nope, v = jnp.split(kv, [QK_NOPE], axis=-1)
    k_pe = apply_rope(k_pe[:, :, None, :], cos, sin)
    q = jnp.concatenate([q_nope, apply_rope(q_pe, cos, sin)], axis=-1)
    k = jnp.concatenate([k_nope, jnp.broadcast_to(k_pe, (B, S, MLA_HEADS, QK_ROPE))], axis=-1)
    return causal_attention(q, k, v).reshape(B, S, MLA_HEADS * V_HEAD)


def swiglu(h, wg, wu, wd):
    return (jax.nn.silu(h @ wg) * (h @ wu)) @ wd


def moe_swiglu(h, router, wg, wu, wd):
    B, S, D = h.shape
    t = h.reshape(B * S, D)
    logits = (t @ router).astype(jnp.float32)
    top_v, top_i = lax.top_k(logits, TOP_K)
    w = jax.nn.softmax(top_v, axis=-1)
    combine = jnp.sum(jax.nn.one_hot(top_i, N_EXPERTS, dtype=jnp.float32) * w[..., None], axis=1)

    def expert(acc, e):
        wg_e, wu_e, wd_e, c_e = e
        return acc + c_e[:, None].astype(t.dtype) * swiglu(t, wg_e, wu_e, wd_e), None

    out, _ = lax.scan(expert, jnp.zeros_like(t), (wg, wu, wd, combine.T))
    return out.reshape(B, S, D)


def setup_inputs(seed: int = 0) -> dict:
    key = jax.random.key(seed)
    ks = iter(jax.random.split(key, 48))
    f32 = jnp.float32

    def nrm(shape, scale):
        return jax.random.normal(next(ks), shape, f32) * scale

    def uni(shape, lo, hi):
        return jax.random.uniform(next(ks), shape, f32, lo, hi)

    L, V, D = DEPTH, DEPTH - 1, D_MODEL
    beta = DEEPNORM_BETA
    return {
        'x': jax.random.normal(next(ks), (BATCH, SEQ, D), f32),
        'w_in': nrm((L, D, IN_COLS), D ** -0.5),
        'w_in_vres': nrm((V, D, VRES_LORA), D ** -0.5),
        'rwkv_mu': uni((L, RWKV_COLS), 0.0, 1.0),
        'rwkv_mu_vres': uni((V, VRES_LORA), 0.0, 1.0),
        'rwkv_w0': uni((L, RWKV_W), -6.0, -1.0),
        'rwkv_w_lora': nrm((L, DECAY_LORA, RWKV_W), 0.1 * DECAY_LORA ** -0.5),
        'rwkv_a0': nrm((L, RWKV_W), 0.5),
        'rwkv_a_lora': nrm((L, ICLR_LORA, RWKV_W), 0.1 * ICLR_LORA ** -0.5),
        'rwkv_g_lora': nrm((L, GATE_LORA, RWKV_W), GATE_LORA ** -0.5),
        'rwkv_v0': 1.0 + nrm((V, RWKV_W), 0.1),
        'rwkv_v_lora': nrm((V, VRES_LORA, RWKV_W), 0.1 * VRES_LORA ** -0.5),
        'rwkv_k_k': 0.85 + nrm((L, RWKV_W), 0.02),
        'rwkv_k_a': 1.0 + nrm((L, RWKV_W), 0.02),
        'rwkv_r_k': nrm((L, RWKV_HEADS, RWKV_HEAD), 0.1),
        'rwkv_gn_g': 1.0 + nrm((L, RWKV_W), 0.02),
        'rwkv_gn_b': nrm((L, RWKV_W), 0.02),
        'mla_q_norm': 1.0 + nrm((L, Q_LORA), 0.02),
        'mla_w_uq': nrm((L, Q_LORA, MLA_HEADS * (QK_NOPE + QK_ROPE)), Q_LORA ** -0.5),
        'mla_kv_norm': 1.0 + nrm((L, KV_LORA), 0.02),
        'mla_w_ukv': nrm((L, KV_LORA, MLA_HEADS * (QK_NOPE + V_HEAD)), KV_LORA ** -0.5),
        'branch_out': nrm((L, N_BRANCH, BRANCH_W, D), beta * BRANCH_W ** -0.5),
        'w_out': nrm((L, D, D), beta * D ** -0.5),
        'ln1_g': 1.0 + nrm((L, D), 0.02),
        'ln1_b': nrm((L, D), 0.02),
        'ffn_wg': nrm((N_DENSE, D, D_FF), beta * D ** -0.5),
        'ffn_wu': nrm((N_DENSE, D, D_FF), beta * D ** -0.5),
        'ffn_wd': nrm((N_DENSE, D_FF, D), beta * D_FF ** -0.5),
        'moe_router': nrm((N_MOE, D, N_EXPERTS), D ** -0.5),
        'moe_wg': nrm((N_MOE, N_EXPERTS, D, D_FF), beta * D ** -0.5),
        'moe_wu': nrm((N_MOE, N_EXPERTS, D, D_FF), beta * D ** -0.5),
        'moe_wd': nrm((N_MOE, N_EXPERTS, D_FF, D), beta * D_FF ** -0.5),
        'ln2_g': 1.0 + nrm((L, D), 0.02),
        'ln2_b': nrm((L, D), 0.02),
    }


def reference(x, w_in, w_in_vres, rwkv_mu, rwkv_mu_vres, rwkv_w0, rwkv_w_lora, rwkv_a0,
              rwkv_a_lora, rwkv_g_lora, rwkv_v0, rwkv_v_lora, rwkv_k_k, rwkv_k_a, rwkv_r_k,
              rwkv_gn_g, rwkv_gn_b, mla_q_norm, mla_w_uq, mla_kv_norm, mla_w_ukv, branch_out,
              w_out, ln1_g, ln1_b, ffn_wg, ffn_wu, ffn_wd, moe_router, moe_wg, moe_wu, moe_wd,
              ln2_g, ln2_b):
    B, S, D = x.shape
    cos_a, sin_a = rope_tables(S, HEAD_DIM)
    cos_m, sin_m = rope_tables(S, QK_ROPE)
    v_first = None
    for l in range(DEPTH):
        if l == 0:
            proj = x @ w_in[0]
        else:
            proj = x @ jnp.concatenate([w_in[l], w_in_vres[l - 1]], axis=1)
        moba_h, rwkv_h, mla_h, gate_h = split_cols(
            proj[..., :IN_COLS], (3 * BRANCH_W, RWKV_COLS, MLA_COLS, N_BRANCH * D_MODEL))
        qa, ka, va = (t.reshape(B, S, MOBA_HEADS, HEAD_DIM) for t in jnp.split(moba_h, 3, axis=-1))
        y_a = moba_attention(apply_rope(qa, cos_a, sin_a), apply_rope(ka, cos_a, sin_a), va)
        y_a = y_a.reshape(B, S, BRANCH_W)
        rwkv_h = token_shift_mix(rwkv_h, rwkv_mu[l])
        if l == 0:
            y_b, v_first = rwkv7_time_mix(rwkv_h, rwkv_w0[l], rwkv_w_lora[l], rwkv_a0[l], rwkv_a_lora[l],
                                          rwkv_g_lora[l], rwkv_k_k[l], rwkv_k_a[l], rwkv_r_k[l],
                                          rwkv_gn_g[l], rwkv_gn_b[l], None, None, None, None)
        else:
            xv_lo = token_shift_mix(proj[..., IN_COLS:], rwkv_mu_vres[l - 1])
            y_b, _ = rwkv7_time_mix(rwkv_h, rwkv_w0[l], rwkv_w_lora[l], rwkv_a0[l], rwkv_a_lora[l],
                                    rwkv_g_lora[l], rwkv_k_k[l], rwkv_k_a[l], rwkv_r_k[l],
                                    rwkv_gn_g[l], rwkv_gn_b[l], v_first, xv_lo,
                                    rwkv_v0[l - 1], rwkv_v_lora[l - 1])
        q_dn, kv_dn, k_pe = split_cols(mla_h, (Q_LORA, KV_LORA, QK_ROPE))
        y_c = mla_attention(q_dn, kv_dn, k_pe, mla_q_norm[l], mla_w_uq[l], mla_kv_norm[l],
                            mla_w_ukv[l], cos_m, sin_m)
        ys = jnp.stack([y_a, y_b, y_c], axis=2)
        branch = jnp.einsum('bsnc,ncd->bsnd', ys, branch_out[l])
        gates = jax.nn.sigmoid(gate_h.reshape(B, S, N_BRANCH, D_MODEL))
        mix = jnp.sum(gates * branch, axis=2) @ w_out[l]
        x = layer_norm(DEEPNORM_ALPHA * x + mix, ln1_g[l], ln1_b[l])
        j = l // 2
        if l % 2 == 0:
            f = swiglu(x, ffn_wg[j], ffn_wu[j], ffn_wd[j])
        else:
            f = moe_swiglu(x, moe_router[j], moe_wg[j], moe_wu[j], moe_wd[j])
        x = layer_norm(DEEPNORM_ALPHA * x + f, ln2_g[l], ln2_b[l])
    return x
```

```python
import functools

import jax
import jax.numpy as jnp
import numpy as np
from jax import lax
from jax.experimental import pallas as pl
from jax.experimental.pallas import tpu as pltpu

F32 = jnp.float32
BF16 = jnp.bfloat16

D_MODEL = 2048
DEPTH = 2
BRANCH_W = D_MODEL // 2
HEAD_DIM = 128
MOBA_HEADS = BRANCH_W // HEAD_DIM
MOBA_BLOCK = 256
MOBA_TOPK = 3
Q_BLOCK = 128
ROPE_THETA = 10000.0
RWKV_HEAD = 64
RWKV_HEADS = BRANCH_W // RWKV_HEAD
RWKV_W = BRANCH_W
DECAY_LORA = 64
ICLR_LORA = 64
VRES_LORA = 32
GATE_LORA = 160
RWKV_GN_EPS = 64e-5
QK_NOPE = 128
QK_ROPE = 64
V_HEAD = 128
MLA_HEADS = BRANCH_W // V_HEAD
Q_LORA = 512
KV_LORA = 512
D_FF = 5632
N_EXPERTS = 8
TOP_K = 2
LN_EPS = 1e-5
RMS_EPS = 1e-6
DEEPNORM_ALPHA = (2 * DEPTH) ** 0.25

LANES = 128
VMEM_LIMIT = 56 * 1024 * 1024


def _params(sem):
    return pltpu.CompilerParams(dimension_semantics=sem, vmem_limit_bytes=VMEM_LIMIT)


def _largest_tile(n, candidates):
    for c in candidates:
        if n % c == 0:
            return c
    return n


def _mm_kernel(x_ref, w_ref, o_ref, acc_ref, *, nk, exact):
    if exact:
        part = jnp.dot(x_ref[...].astype(F32), w_ref[...].astype(F32),
                       preferred_element_type=F32, precision=lax.Precision.HIGHEST)
    else:
        part = jnp.dot(x_ref[...].astype(BF16), w_ref[...].astype(BF16),
                       preferred_element_type=F32)
    if nk == 1:
        o_ref[...] = part.astype(o_ref.dtype)
        return
    k = pl.program_id(2)

    @pl.when(k == 0)
    def _():
        acc_ref[...] = part

    @pl.when(k > 0)
    def _():
        acc_ref[...] += part

    @pl.when(k == nk - 1)
    def _():
        o_ref[...] = acc_ref[...].astype(o_ref.dtype)


def matmul(x, w, *, out_dtype=F32, exact=False, name="mm"):
    m, k = x.shape
    k2, n = w.shape
    assert k == k2
    tm = _largest_tile(m, (512, 256, 128, 64, 32, 16, 8))
    tn = _largest_tile(n, (512, 256, 128)) if n % LANES == 0 else n
    tk = k if k <= 2048 else _largest_tile(k, (2048, 1792, 1536, 1408, 1280, 1024, 512, 256, 128))
    nk = k // tk
    return pl.pallas_call(
        functools.partial(_mm_kernel, nk=nk, exact=exact),
        grid=(m // tm, n // tn, nk),
        in_specs=[pl.BlockSpec((tm, tk), lambda i, j, kk: (i, kk)),
                  pl.BlockSpec((tk, tn), lambda i, j, kk: (kk, j))],
        out_specs=pl.BlockSpec((tm, tn), lambda i, j, kk: (i, j)),
        out_shape=jax.ShapeDtypeStruct((m, n), out_dtype),
        scratch_shapes=[pltpu.VMEM((tm, tn), F32)],
        compiler_params=_params(("parallel", "parallel", "arbitrary")),
        name=name,
    )(x, w)


def _rope_full(x, c, s):
    return x * c + pltpu.roll(x, HEAD_DIM // 2, 1) * s


def _moba_kernel(q_ref, k_ref, v_ref, cos_ref, sin_ref, o_ref, kr_ref, km_ref, *, seq):
    qi = pl.program_id(2)
    nblk = seq // MOBA_BLOCK

    @pl.when(qi == 0)
    def _():
        kr = _rope_full(k_ref[...], cos_ref[...], sin_ref[...])
        kr_ref[...] = kr.astype(BF16)
        km_ref[...] = jnp.zeros_like(km_ref)
        for n in range(nblk):
            km_ref[n:n + 1, :] = jnp.mean(kr[n * MOBA_BLOCK:(n + 1) * MOBA_BLOCK, :],
                                          axis=0, keepdims=True)

    start = pl.multiple_of(qi * Q_BLOCK, Q_BLOCK)
    q = _rope_full(q_ref[...], cos_ref[pl.ds(start, Q_BLOCK), :], sin_ref[pl.ds(start, Q_BLOCK), :])
    gate = lax.dot_general(q, km_ref[...], (((1,), (1,)), ((), ())),
                           preferred_element_type=F32, precision=lax.Precision.HIGHEST)
    qpos = start + lax.broadcasted_iota(jnp.int32, (Q_BLOCK, 1), 0)
    own = lax.shift_right_logical(qpos, int(np.log2(MOBA_BLOCK)))
    col = lax.broadcasted_iota(jnp.int32, (Q_BLOCK, LANES), 1)
    gate = jnp.where(col < own, gate, -jnp.inf)
    cnt = jnp.zeros((Q_BLOCK, LANES), jnp.int32)
    for m in range(nblk):
        gm = gate[:, m:m + 1]
        beats = (gm > gate) | ((gm == gate) & (col > m))
        cnt = cnt + beats.astype(jnp.int32)
    selected = jnp.where((cnt < MOBA_TOPK) & (gate > -jnp.inf), 1, 0)

    scale = HEAD_DIM ** -0.5
    qb = q.astype(BF16)
    scores = []
    for n in range(nblk):
        kb = kr_ref[n * MOBA_BLOCK:(n + 1) * MOBA_BLOCK, :]
        s = lax.dot_general(qb, kb, (((1,), (1,)), ((), ())), preferred_element_type=F32) * scale
        kpos = n * MOBA_BLOCK + lax.broadcasted_iota(jnp.int32, (1, MOBA_BLOCK), 1)
        sel_n = selected[:, n:n + 1] > 0
        allowed = ((own > n) & sel_n) | ((own == n) & (kpos <= qpos))
        scores.append(jnp.where(allowed, s, -jnp.inf))
    mx = scores[0].max(axis=1, keepdims=True)
    for n in range(1, nblk):
        mx = jnp.maximum(mx, scores[n].max(axis=1, keepdims=True))
    acc = jnp.zeros((Q_BLOCK, HEAD_DIM), F32)
    den = jnp.zeros((Q_BLOCK, 1), F32)
    for n in range(nblk):
        p = jnp.exp(scores[n] - mx)
        den = den + p.sum(axis=1, keepdims=True)
        vb = v_ref[n * MOBA_BLOCK:(n + 1) * MOBA_BLOCK, :].astype(BF16)
        acc = acc + jnp.dot(p.astype(BF16), vb, preferred_element_type=F32)
    o_ref[...] = acc / den


def moba_attention(qkv, cos_t, sin_t, batch, seq):
    nq = seq // Q_BLOCK
    h = MOBA_HEADS
    return pl.pallas_call(
        functools.partial(_moba_kernel, seq=seq),
        grid=(batch, h, nq),
        in_specs=[pl.BlockSpec((Q_BLOCK, HEAD_DIM), lambda b, hh, i: (b * nq + i, hh)),
                  pl.BlockSpec((seq, HEAD_DIM), lambda b, hh, i: (b, h + hh)),
                  pl.BlockSpec((seq, HEAD_DIM), lambda b, hh, i: (b, 2 * h + hh)),
                  pl.BlockSpec((seq, HEAD_DIM), lambda b, hh, i: (0, 0)),
                  pl.BlockSpec((seq, HEAD_DIM), lambda b, hh, i: (0, 0))],
        out_specs=pl.BlockSpec((Q_BLOCK, HEAD_DIM), lambda b, hh, i: (b * nq + i, hh)),
        out_shape=jax.ShapeDtypeStruct((batch * seq, BRANCH_W), F32),
        scratch_shapes=[pltpu.VMEM((seq, HEAD_DIM), BF16), pltpu.VMEM((LANES, HEAD_DIM), F32)],
        compiler_params=_params(("parallel", "parallel", "arbitrary")),
        name="moba_attn",
    )(qkv, qkv, qkv, cos_t, sin_t)


MLA_TQ = 256
MLA_QK = 2 * LANES


def _mla_kernel(q_ref, kn_ref, kp_ref, v_ref, o_ref):
    qi = pl.program_id(2)
    tq = MLA_TQ
    scale = (QK_NOPE + QK_ROPE) ** -0.5
    q = q_ref[...]
    qpos = qi * tq + lax.broadcasted_iota(jnp.int32, (tq, 1), 0)

    def body(j, carry):
        m, l, acc = carry
        off = pl.multiple_of(j * tq, tq)
        kcat = jnp.concatenate([kn_ref[pl.ds(off, tq), :], kp_ref[pl.ds(off, tq), :]], axis=1)
        s = lax.dot_general(q, kcat, (((1,), (1,)), ((), ())), preferred_element_type=F32) * scale
        kpos = off + lax.broadcasted_iota(jnp.int32, (1, tq), 1)
        s = jnp.where(kpos <= qpos, s, -jnp.inf)
        m_new = jnp.maximum(m, s.max(axis=1, keepdims=True))
        a = jnp.exp(m - m_new)
        p = jnp.exp(s - m_new)
        l = a * l + p.sum(axis=1, keepdims=True)
        acc = a * acc + jnp.dot(p.astype(BF16), v_ref[pl.ds(off, tq), :], preferred_element_type=F32)
        return m_new, l, acc

    init = (jnp.full((tq, 1), -jnp.inf, F32), jnp.zeros((tq, 1), F32), jnp.zeros((tq, V_HEAD), F32))
    m, l, acc = lax.fori_loop(0, qi + 1, body, init)
    o_ref[...] = acc / l


def mla_attention(q_cat, kv, kpe, batch, seq):
    nq = seq // MLA_TQ
    h = MLA_HEADS
    return pl.pallas_call(
        _mla_kernel,
        grid=(batch, h, nq),
        in_specs=[pl.BlockSpec((MLA_TQ, MLA_QK), lambda b, hh, i: (b * nq + i, hh)),
                  pl.BlockSpec((seq, QK_NOPE), lambda b, hh, i: (b, 2 * hh)),
                  pl.BlockSpec((seq, LANES), lambda b, hh, i: (b, 0)),
                  pl.BlockSpec((seq, V_HEAD), lambda b, hh, i: (b, 2 * hh + 1))],
        out_specs=pl.BlockSpec((MLA_TQ, V_HEAD), lambda b, hh, i: (b * nq + i, hh)),
        out_shape=jax.ShapeDtypeStruct((batch * seq, BRANCH_W), F32),
        compiler_params=_params(("parallel", "parallel", "arbitrary")),
        name="mla_attn",
    )(q_cat, kv, kpe, kv)


SCAN_TB = 32
SCAN_ROWS = 2


def _wkv_kernel(w_ref, nkk_ref, b_ref, k_ref, r_ref, v_ref, y_ref, s_ref, *, n_val):
    @pl.when(pl.program_id(0) == 0)
    def _():
        s_ref[...] = jnp.zeros_like(s_ref)

    for i0 in range(0, n_val, SCAN_ROWS):
        rows = tuple(range(i0, i0 + SCAN_ROWS))

        def step(t, states, rows=rows):
            w = w_ref[t]
            nkk = nkk_ref[t]
            b = b_ref[t]
            k = k_ref[t]
            r = r_ref[t]
            new = []
            for i, s in zip(rows, states):
                sa = jnp.sum(s * nkk, axis=0, keepdims=True)
                s = s * w + sa * b + v_ref[t, i:i + 1, :] * k
                y_ref[t, i:i + 1, :] = jnp.sum(s * r, axis=0, keepdims=True)
                new.append(s)
            return tuple(new)

        out = lax.fori_loop(0, SCAN_TB, step, tuple(s_ref[i] for i in rows))
        for i, s in zip(rows, out):
            s_ref[i] = s


def wkv7_scan(w, nkk, b, k, r, v):
    seq, nkey, lanes = w.shape
    n_val = v.shape[1]
    vec = pl.BlockSpec((SCAN_TB, nkey, lanes), lambda t: (t, 0, 0))
    val = pl.BlockSpec((SCAN_TB, n_val, lanes), lambda t: (t, 0, 0))
    return pl.pallas_call(
        functools.partial(_wkv_kernel, n_val=n_val),
        grid=(seq // SCAN_TB,),
        in_specs=[vec, vec, vec, vec, vec, val],
        out_specs=val,
        out_shape=jax.ShapeDtypeStruct((seq, n_val, lanes), F32),
        scratch_shapes=[pltpu.VMEM((n_val, nkey, lanes), F32)],
        compiler_params=_params(("arbitrary",)),
        name="wkv7_scan",
    )(w, nkk, b, k, r, v)


FFN_TM = 512
FFN_TF = 512


def _ffn_kernel(te_ref, tb_ref, tv_ref, x_ref, cw_ref, wg_ref, wu_ref, wd_ref, o_ref):
    i = pl.program_id(0)
    f = pl.program_id(1)

    @pl.when(f == 0)
    def _():
        o_ref[...] = jnp.zeros_like(o_ref)

    @pl.when(tv_ref[i] > 0)
    def _():
        x = x_ref[...]
        g = jnp.dot(x, wg_ref[0].astype(BF16), preferred_element_type=F32)
        u = jnp.dot(x, wu_ref[0].astype(BF16), preferred_element_type=F32)
        act = g * (1.0 / (1.0 + jnp.exp(-g))) * u * cw_ref[...]
        o_ref[...] += jnp.dot(act.astype(BF16), wd_ref[0].astype(BF16), preferred_element_type=F32)


def ffn_tiles(x, cw, wg, wu, wd, tile_expert, tile_block, tile_valid):
    p, d = x.shape
    _, _, dff = wg.shape
    n_tiles = tile_expert.shape[0]
    grid_spec = pltpu.PrefetchScalarGridSpec(
        num_scalar_prefetch=3,
        grid=(n_tiles, dff // FFN_TF),
        in_specs=[pl.BlockSpec((FFN_TM, d), lambda i, f, te, tb, tv: (tb[i], 0)),
                  pl.BlockSpec((FFN_TM, 1), lambda i, f, te, tb, tv: (tb[i], 0)),
                  pl.BlockSpec((1, d, FFN_TF), lambda i, f, te, tb, tv: (te[i], 0, f)),
                  pl.BlockSpec((1, d, FFN_TF), lambda i, f, te, tb, tv: (te[i], 0, f)),
                  pl.BlockSpec((1, FFN_TF, d), lambda i, f, te, tb, tv: (te[i], f, 0))],
        out_specs=pl.BlockSpec((FFN_TM, d), lambda i, f, te, tb, tv: (i, 0)),
    )
    return pl.pallas_call(
        _ffn_kernel,
        grid_spec=grid_spec,
        out_shape=jax.ShapeDtypeStruct((n_tiles * FFN_TM, d), F32),
        compiler_params=_params(("parallel", "arbitrary")),
        name="swiglu_tiles",
    )(tile_expert, tile_block, tile_valid, x, cw, wg, wu, wd)


def _layer_norm(x, g, b):
    mu = x.mean(-1, keepdims=True)
    var = jnp.square(x - mu).mean(-1, keepdims=True)
    return (x - mu) * lax.rsqrt(var + LN_EPS) * g + b


def _rms_norm(x, g):
    return x * lax.rsqrt(jnp.square(x).mean(-1, keepdims=True) + RMS_EPS) * g


def _rope_tables(seq, dim):
    inv = 1.0 / (ROPE_THETA ** (jnp.arange(0, dim, 2, dtype=F32) / dim))
    ang = jnp.arange(seq, dtype=F32)[:, None] * inv[None, :]
    return jnp.cos(ang), jnp.sin(ang)


def _token_shift(h, mu):
    prev = jnp.pad(h, ((0, 0), (1, 0), (0, 0)))[:, :-1]
    return h + (prev - h) * mu


def _to_chain_lanes(t, batch, seq, dup):
    t = t.reshape(batch, seq, RWKV_HEADS, RWKV_HEAD).transpose(1, 3, 0, 2).reshape(seq, RWKV_HEAD, batch * RWKV_HEADS)
    return jnp.concatenate([t] * dup, axis=-1) if dup > 1 else t


def _rwkv_mix(rwkv_h, xv_lo, v_first, prm, batch, seq):
    t = batch * seq
    h2 = rwkv_h.reshape(t, -1)
    r, k, v = (h2[:, i * RWKV_W:(i + 1) * RWKV_W] for i in range(3))
    o = 3 * RWKV_W
    xw = h2[:, o:o + DECAY_LORA]
    xa = h2[:, o + DECAY_LORA:o + DECAY_LORA + ICLR_LORA]
    xg = h2[:, o + DECAY_LORA + ICLR_LORA:]
    w_log = -jax.nn.softplus(-(prm["w0"] + matmul(jnp.tanh(xw), prm["w_lora"], name="rwkv_w_lora"))) - 0.5
    decay = jnp.exp(-jnp.exp(w_log))
    a = jax.nn.sigmoid(prm["a0"] + matmul(xa, prm["a_lora"], name="rwkv_a_lora"))
    g = matmul(jax.nn.sigmoid(xg), prm["g_lora"], name="rwkv_g_lora")
    v_raw = v
    if v_first is not None:
        v = v + (v_first - v) * jax.nn.sigmoid(prm["v0"] + matmul(xv_lo, prm["v_lora"], name="rwkv_v_lora"))
    hd = lambda z: z.reshape(t, RWKV_HEADS, RWKV_HEAD)
    kk = hd(k * prm["k_k"])
    kk = kk / jnp.maximum(jnp.sqrt(jnp.sum(kk * kk, axis=-1, keepdims=True)), 1e-12)
    kk = kk.reshape(t, RWKV_W)
    k2 = k * (1.0 + (a - 1.0) * prm["k_a"])
    lanes_per_half = batch * RWKV_HEADS
    dup = LANES // lanes_per_half
    n_val = RWKV_HEAD // dup
    tl = functools.partial(_to_chain_lanes, batch=batch, seq=seq)
    v_l = tl(v, dup=1).reshape(seq, dup, n_val, lanes_per_half).transpose(0, 2, 1, 3).reshape(seq, n_val, LANES)
    y_l = wkv7_scan(tl(decay, dup=dup), tl(-kk, dup=dup), tl(kk * a, dup=dup), tl(k2, dup=dup), tl(r, dup=dup), v_l)
    y = y_l.reshape(seq, n_val, dup, batch, RWKV_HEADS).transpose(3, 0, 4, 2, 1).reshape(t, RWKV_HEADS, RWKV_HEAD)
    mu = y.mean(-1, keepdims=True)
    var = jnp.square(y - mu).mean(-1, keepdims=True)
    y = ((y - mu) * lax.rsqrt(var + RWKV_GN_EPS)).reshape(t, RWKV_W) * prm["gn_g"] + prm["gn_b"]
    bonus = (jnp.sum(hd(r) * hd(k2) * prm["r_k"], axis=-1, keepdims=True) * hd(v)).reshape(t, RWKV_W)
    return (y + bonus) * g, v_raw


def _mla_mix(q_dn, kv_dn, k_pe, prm, cos_m, sin_m, batch, seq):
    t = batch * seq
    half = QK_ROPE // 2
    w_uq = prm["w_uq"].reshape(Q_LORA, MLA_HEADS, QK_NOPE + QK_ROPE)
    q = matmul(_rms_norm(q_dn, prm["q_norm"]), w_uq.reshape(Q_LORA, -1), name="mla_uq")
    q = q.reshape(batch, seq, MLA_HEADS, QK_NOPE + QK_ROPE)
    q_nope, q_pe = q[..., :QK_NOPE], q[..., QK_NOPE:]
    c = cos_m[None, :, None, :]
    s = sin_m[None, :, None, :]
    q1, q2 = q_pe[..., :half], q_pe[..., half:]
    q_pe = jnp.concatenate([q1 * c - q2 * s, q2 * c + q1 * s], axis=-1)
    q_cat = jnp.concatenate([q_nope, q_pe, jnp.zeros((batch, seq, MLA_HEADS, MLA_QK - QK_NOPE - QK_ROPE), F32)], axis=-1)
    q_cat = q_cat.reshape(t, MLA_HEADS * MLA_QK).astype(BF16)
    kv = matmul(_rms_norm(kv_dn, prm["kv_norm"]), prm["w_ukv"], out_dtype=BF16, name="mla_ukv")
    kp = k_pe.reshape(batch, seq, QK_ROPE)
    k1, k2 = kp[..., :half], kp[..., half:]
    c2, s2 = cos_m[None], sin_m[None]
    kp = jnp.concatenate([k1 * c2 - k2 * s2, k2 * c2 + k1 * s2, jnp.zeros((batch, seq, LANES - QK_ROPE), F32)], axis=-1)
    return mla_attention(q_cat, kv, kp.reshape(t, LANES).astype(BF16), batch, seq)


def _dense_ffn(x2, wg, wu, wd):
    t = x2.shape[0]
    n_tiles = t // FFN_TM
    ar = jnp.arange(n_tiles, dtype=jnp.int32)
    return ffn_tiles(x2.astype(BF16), jnp.ones((t, 1), F32), wg[None], wu[None], wd[None],
                     jnp.zeros((n_tiles,), jnp.int32), ar, jnp.ones((n_tiles,), jnp.int32))


def _moe_ffn(x2, router, wg, wu, wd):
    t, d = x2.shape
    logits = matmul(x2, router, exact=True, name="moe_router")
    top_v, top_i = lax.top_k(logits, TOP_K)
    wts = jax.nn.softmax(top_v, axis=-1)
    n_assign = t * TOP_K
    eid = top_i.reshape(n_assign).astype(jnp.int32)
    order = jnp.argsort(eid, stable=True).astype(jnp.int32)
    counts = jnp.zeros((N_EXPERTS,), jnp.int32).at[eid].add(1)
    padded = ((counts + FFN_TM - 1) // FFN_TM) * FFN_TM
    starts_p = jnp.cumsum(padded) - padded
    starts_s = jnp.cumsum(counts) - counts
    n_tiles = n_assign // FFN_TM + N_EXPERTS
    p_rows = n_tiles * FFN_TM
    sorted_e = eid[order]
    rank_in_group = jnp.arange(n_assign, dtype=jnp.int32) - starts_s[sorted_e]
    slot_of_sorted = starts_p[sorted_e] + rank_in_group
    slot_token = jnp.zeros((p_rows,), jnp.int32).at[slot_of_sorted].set(order // TOP_K)
    slot_w = jnp.zeros((p_rows,), F32).at[slot_of_sorted].set(wts.reshape(n_assign)[order])
    slot_of_assign = jnp.zeros((n_assign,), jnp.int32).at[order].set(slot_of_sorted)
    tile_start = jnp.arange(n_tiles, dtype=jnp.int32) * FFN_TM
    ends_p = starts_p + padded
    tile_e = jnp.sum((tile_start[:, None] >= ends_p[None, :]).astype(jnp.int32), axis=1)
    n_used = jnp.sum(padded) // FFN_TM
    tile_valid = (jnp.arange(n_tiles) < n_used).astype(jnp.int32)
    last = jnp.maximum(n_used - 1, 0)
    tile_e = jnp.where(tile_valid > 0, jnp.minimum(tile_e, N_EXPERTS - 1), tile_e[last]).astype(jnp.int32)
    tile_blk = jnp.where(tile_valid > 0, jnp.arange(n_tiles), last).astype(jnp.int32)
    xs = jnp.take(x2.astype(BF16), slot_token, axis=0)
    ys = ffn_tiles(xs, slot_w[:, None], wg, wu, wd, tile_e, tile_blk, tile_valid)
    sl = slot_of_assign.reshape(t, TOP_K)
    return jnp.take(ys, sl[:, 0], axis=0) + jnp.take(ys, sl[:, 1], axis=0)


def kernel(x, w_in, w_in_vres, rwkv_mu, rwkv_mu_vres, rwkv_w0, rwkv_w_lora, rwkv_a0, rwkv_a_lora, rwkv_g_lora, rwkv_v0, rwkv_v_lora, rwkv_k_k, rwkv_k_a, rwkv_r_k, rwkv_gn_g, rwkv_gn_b, mla_q_norm, mla_w_uq, mla_kv_norm, mla_w_ukv, branch_out, w_out, ln1_g, ln1_b, ffn_wg, ffn_wu, ffn_wd, moe_router, moe_wg, moe_wu, moe_wd, ln2_g, ln2_b):
    batch, seq, d = x.shape
    t = batch * seq
    cos_a, sin_a = _rope_tables(seq, HEAD_DIM)
    cos_t = jnp.concatenate([cos_a, cos_a], axis=-1)
    sin_t = jnp.concatenate([-sin_a, sin_a], axis=-1)
    cos_m, sin_m = _rope_tables(seq, QK_ROPE)
    c_moba = 3 * BRANCH_W
    c_rwkv = c_moba + 3 * RWKV_W + DECAY_LORA + ICLR_LORA + GATE_LORA
    c_mla = c_rwkv + Q_LORA + KV_LORA + QK_ROPE
    x2 = x.reshape(t, d)
    v_first = None
    for l in range(DEPTH):
        w = w_in[l]
        xb = x2.astype(BF16)
        qkv = matmul(xb, w[:, :c_moba], name="in_moba")
        rkv = matmul(xb, w[:, c_moba:c_moba + 3 * RWKV_W], name="in_rwkv")
        w_small = [w[:, c_moba + 3 * RWKV_W:c_rwkv], w[:, c_rwkv + Q_LORA + KV_LORA:c_mla]]
        if l > 0:
            w_small.append(w_in_vres[l - 1])
        small = matmul(xb, jnp.concatenate(w_small, axis=1), name="in_small")
        n_lora = DECAY_LORA + ICLR_LORA + GATE_LORA
        mla_dn = matmul(xb, w[:, c_rwkv:c_rwkv + Q_LORA + KV_LORA], name="in_mla")
        gate_h = matmul(xb, w[:, c_mla:], name="in_gate")
        y_a = moba_attention(qkv, cos_t, sin_t, batch, seq)
        rwkv_h = _token_shift(jnp.concatenate([rkv, small[:, :n_lora]], axis=1).reshape(batch, seq, -1), rwkv_mu[l])
        prm = dict(w0=rwkv_w0[l], w_lora=rwkv_w_lora[l], a0=rwkv_a0[l], a_lora=rwkv_a_lora[l],
                   g_lora=rwkv_g_lora[l], k_k=rwkv_k_k[l], k_a=rwkv_k_a[l], r_k=rwkv_r_k[l],
                   gn_g=rwkv_gn_g[l], gn_b=rwkv_gn_b[l])
        if l == 0:
            y_b, v_first = _rwkv_mix(rwkv_h, None, None, prm, batch, seq)
        else:
            xv_lo = _token_shift(small[:, n_lora + QK_ROPE:].reshape(batch, seq, VRES_LORA), rwkv_mu_vres[l - 1])
            prm.update(v0=rwkv_v0[l - 1], v_lora=rwkv_v_lora[l - 1])
            y_b, _ = _rwkv_mix(rwkv_h, xv_lo.reshape(t, VRES_LORA), v_first, prm, batch, seq)
        mprm = dict(q_norm=mla_q_norm[l], w_uq=mla_w_uq[l], kv_norm=mla_kv_norm[l], w_ukv=mla_w_ukv[l])
        y_c = _mla_mix(mla_dn[:, :Q_LORA], mla_dn[:, Q_LORA:], small[:, n_lora:n_lora + QK_ROPE], mprm,
                       cos_m, sin_m, batch, seq)
        mixin = jnp.zeros((t, d), F32)
        for n, y in enumerate((y_a, y_b, y_c)):
            br = matmul(y, branch_out[l, n], name=f"branch_{n}")
            mixin = mixin + jax.nn.sigmoid(gate_h[:, n * d:(n + 1) * d]) * br
        mix = matmul(mixin, w_out[l], name="w_out")
        x2 = _layer_norm(DEEPNORM_ALPHA * x2 + mix, ln1_g[l], ln1_b[l])
        j = l // 2
        if l % 2 == 0:
            f = _dense_ffn(x2, ffn_wg[j], ffn_wu[j], ffn_wd[j])
        else:
            f = _moe_ffn(x2, moe_router[j], moe_wg[j], moe_wu[j], moe_wd[j])
        x2 = _layer_norm(DEEPNORM_ALPHA * x2 + f, ln2_g[l], ln2_b[l])
    return x2.reshape(batch, seq, d)
```
